```python
import math
import jax, jax.numpy as jnp
from jax import lax
import numpy as np

D_MODEL = 1024
BATCH = 1
SEQ = 16384
DEPTH = 2

GRID_W = 64
CTX_LEN = 256
HEAD_DIM = 64
ROPE_THETA = 10000.0
ROPE_FREQS = HEAD_DIM // 4
Q_BLOCK = 128
RMS_EPS = 1e-6

CONV_W = D_MODEL // 2
GQA_HEADS = 8
KV_HEADS = 2
GQA_W = GQA_HEADS * HEAD_DIM
KV_W = KV_HEADS * HEAD_DIM
EV_IN = 3 * CONV_W + GQA_W + 2 * KV_W
EV_SPLITS = [CONV_W, 2 * CONV_W, 3 * CONV_W, 3 * CONV_W + GQA_W, 3 * CONV_W + GQA_W + KV_W]
EV_KV_OFF = 3 * CONV_W + GQA_W
EV_OUT = CONV_W + GQA_W

S5_W = D_MODEL // 2
S5_GROUP_CH = 16
S5_GROUPS = S5_W // S5_GROUP_CH
S5_STATE = 64
S5_DT_MIN = 1e-3
S5_DT_MAX = 1e-1
DIFF_HEADS = 4
DIFF_W = DIFF_HEADS * 2 * HEAD_DIM
OD_IN = S5_W + 3 * DIFF_W
OD_SPLITS = [S5_W, S5_W + DIFF_W, S5_W + 2 * DIFF_W]
OD_K_OFF = S5_W + DIFF_W
OD_OUT = S5_W + DIFF_W

D_FF = 2816

N_EVEN = (DEPTH + 1) // 2
N_ODD = DEPTH // 2
F32 = jnp.float32

kernel_name = 'hybrid_conv_gqa_s5_diffattn_prefix_block'


def rmsnorm(x, g, eps=RMS_EPS):
    xf = x.astype(F32)
    y = xf * lax.rsqrt(jnp.mean(xf * xf, axis=-1, keepdims=True) + eps)
    return (y * g.astype(F32)).astype(x.dtype)


def modulate(x, g, shift, scale):
    return rmsnorm(x, g) * (1 + scale) + shift


def dwconv3(x, w):
    xp = jnp.pad(x, ((0, 0), (1, 1), (0, 0)))
    return xp[:, :-2] * w[0] + xp[:, 1:-1] * w[1] + xp[:, 2:] * w[2]


def rope_tables(rows):
    row = jnp.repeat(jnp.arange(rows, dtype=F32), GRID_W)
    col = jnp.tile(jnp.arange(GRID_W, dtype=F32), rows)
    inv = ROPE_THETA ** (-jnp.arange(ROPE_FREQS, dtype=F32) / ROPE_FREQS)
    ang = jnp.concatenate([row[:, None] * inv, col[:, None] * inv], axis=-1)
    return jnp.cos(ang), jnp.sin(ang)


def apply_rope(x, cos, sin):
    half = x.shape[-1] // 2
    bshape = (1, x.shape[1]) + (1,) * (x.ndim - 3) + (half,)
    cs = cos.reshape(bshape)
    sn = sin.reshape(bshape)
    xf = x.astype(F32).reshape(x.shape[:-1] + (half, 2))
    xe, xo = xf[..., 0], xf[..., 1]
    out = jnp.stack([xe * cs - xo * sn, xe * sn + xo * cs], axis=-1)
    return out.reshape(x.shape).astype(x.dtype)


def _blocks(x):
    b, l = x.shape[:2]
    return jnp.moveaxis(x.reshape((b, l // Q_BLOCK, Q_BLOCK) + x.shape[2:]), 1, 0)


def _unblocks(y):
    nb, b, qb = y.shape[:3]
    return jnp.moveaxis(y, 0, 1).reshape((b, nb * qb) + y.shape[3:])


def gqa_attention(q, k, v):
    b, l, h, d = q.shape
    kvh = k.shape[2]
    qg = q.reshape(b, l, kvh, h // kvh, d)
    scale = d ** -0.5

    def one_block(qb):
        s = jnp.einsum('bqkgd,bskd->bkgqs', qb, k).astype(F32) * scale
        p = jax.nn.softmax(s, axis=-1).astype(v.dtype)
        return jnp.einsum('bkgqs,bskd->bqkgd', p, v)

    o = _unblocks(lax.map(one_block, _blocks(qg)))
    return o.reshape(b, l, h * d)


def diff_attention(q, k, v, lam):
    scale = q.shape[-1] ** -0.5

    def one_block(qb):
        s = jnp.einsum('bqhcd,bshcd->bhcqs', qb, k).astype(F32) * scale
        p = jax.nn.softmax(s, axis=-1)
        a = (p[:, :, 0] - lam * p[:, :, 1]).astype(v.dtype)
        return jnp.einsum('bhqs,bshe->bqhe', a, v)

    return _unblocks(lax.map(one_block, _blocks(q)))


def s5_discretise(a_re, a_im, log_dt, b_re, b_im):
    dt = jnp.exp(log_dt)[:, None]
    mag = jnp.exp(a_re * dt)
    ab_re = mag * jnp.cos(a_im * dt)
    ab_im = mag * jnp.sin(a_im * dt)
    nr, ni = ab_re - 1.0, ab_im
    den = a_re * a_re + a_im * a_im
    cr = ((nr * a_re + ni * a_im) / den)[..., None]
    ci = ((ni * a_re - nr * a_im) / den)[..., None]
    bb_re = cr * b_re - ci * b_im
    bb_im = cr * b_im + ci * b_re
    return ab_re, ab_im, bb_re, bb_im


def _complex_affine_combine(e1, e2):
    a1r, a1i, b1r, b1i = e1
    a2r, a2i, b2r, b2i = e2
    return (a2r * a1r - a2i * a1i,
            a2r * a1i + a2i * a1r,
            a2r * b1r - a2i * b1i + b2r,
            a2r * b1i + a2i * b1r + b2i)


def s5_scan(bu_re, bu_im, ab_re, ab_im, h0, reverse):
    if h0 is not None:
        h0r, h0i = h0
        edge = -1 if reverse else 0
        bu_re = bu_re.at[:, edge].add(ab_re * h0r - ab_im * h0i)
        bu_im = bu_im.at[:, edge].add(ab_re * h0i + ab_im * h0r)
    a_re = jnp.broadcast_to(ab_re, bu_re.shape)
    a_im = jnp.broadcast_to(ab_im, bu_im.shape)
    _, _, hr, hi = lax.associative_scan(_complex_affine_combine, (a_re, a_im, bu_re, bu_im),
                                        reverse=reverse, axis=1)
    return hr, hi


def s5_input(ug, bb_re, bb_im):
    return (jnp.einsum('blgh,gph->blgp', ug, bb_re), jnp.einsum('blgh,gph->blgp', ug, bb_im))


def s5_readout(hr, hi, c_re, c_im):
    b, l, g, _ = hr.shape
    y = jnp.einsum('gcp,blgp->blgc', c_re, hr) - jnp.einsum('gcp,blgp->blgc', c_im, hi)
    return y.reshape(b, l, g * c_re.shape[1])


def s5_glu(y, u, d_skip, glu_w):
    z = jax.nn.gelu(y + d_skip.astype(F32) * u.astype(F32)).astype(u.dtype)
    val, gate = jnp.split(z @ glu_w, 2, axis=-1)
    return val * jax.nn.sigmoid(gate)


def s5_bidirectional(u, uc, a_re, a_im, log_dt, b_re, b_im, c_re, c_im, d_skip, glu_w, ctx_out):
    b, l, _ = u.shape
    ug = u.astype(F32).reshape(b, l, S5_GROUPS, S5_GROUP_CH)
    ucg = uc.astype(F32).reshape(b, uc.shape[1], S5_GROUPS, S5_GROUP_CH)
    ys, ycs = [], []
    for direction in range(2):
        rev = direction == 1
        ab_re, ab_im, bb_re, bb_im = s5_discretise(
            a_re[direction].astype(F32), a_im[direction].astype(F32), log_dt[direction].astype(F32),
            b_re[direction].astype(F32), b_im[direction].astype(F32))
        cr, ci = c_re[direction].astype(F32), c_im[direction].astype(F32)
        hcr, hci = s5_scan(*s5_input(ucg, bb_re, bb_im), ab_re, ab_im, None, rev)
        edge = 0 if rev else -1
        hr, hi = s5_scan(*s5_input(ug, bb_re, bb_im), ab_re, ab_im, (hcr[:, edge], hci[:, edge]), rev)
        ys.append(s5_readout(hr, hi, cr, ci))
        if ctx_out:
            ycs.append(s5_readout(hcr, hci, cr, ci))
    y = s5_glu(ys[0] + ys[1], u, d_skip, glu_w)
    yc = s5_glu(ycs[0] + ycs[1], uc, d_skip, glu_w) if ctx_out else None
    return y, yc


def conv_ffn(h, w_up, conv_w, conv_b, w_down):
    u = dwconv3(h @ w_up, conv_w) + conv_b
    a, g = jnp.split(u, 2, axis=-1)
    return (a * jax.nn.silu(g)) @ w_down


def mixer_conv_gqa(h, hc, w_in, conv_w, q_g, k_g, w_out, cos, sin, ctx_out):
    b, l, _ = h.shape
    cl = hc.shape[1]
    gb, gc, hv, q, k, v = jnp.split(h @ w_in, EV_SPLITS, axis=-1)
    y_conv = gb * dwconv3(gc * hv, conv_w)
    q = apply_rope(rmsnorm(q.reshape(b, l, GQA_HEADS, HEAD_DIM), q_g), cos, sin)
    k = apply_rope(rmsnorm(k.reshape(b, l, KV_HEADS, HEAD_DIM), k_g), cos, sin)
    v = v.reshape(b, l, KV_HEADS, HEAD_DIM)
    if ctx_out:
        cgb, cgc, chv, cq, ck, cv = jnp.split(hc @ w_in, EV_SPLITS, axis=-1)
    else:
        ck, cv = jnp.split(hc @ w_in[:, EV_KV_OFF:], [KV_W], axis=-1)
    kc = rmsnorm(ck.reshape(b, cl, KV_HEADS, HEAD_DIM), k_g)
    vc = cv.reshape(b, cl, KV_HEADS, HEAD_DIM)
    y_att = gqa_attention(q, jnp.concatenate([kc, k], axis=1), jnp.concatenate([vc, v], axis=1))
    y = jnp.concatenate([y_conv, y_att], axis=-1) @ w_out
    if not ctx_out:
        return y, None
    yc_conv = cgb * dwconv3(cgc * chv, conv_w)
    qc = rmsnorm(cq.reshape(b, cl, GQA_HEADS, HEAD_DIM), q_g)
    yc_att = gqa_attention(qc, kc, vc)
    yc = jnp.concatenate([yc_conv, yc_att], axis=-1) @ w_out
    return y, yc


def mixer_s5_diff(h, hc, w_in, a_re, a_im, log_dt, b_re, b_im, c_re, c_im, d_skip, glu_w,
                  lam_q1, lam_k1, lam_q2, lam_k2, subln_g, w_out, cos, sin, lam_init, ctx_out):
    b, l, _ = h.shape
    cl = hc.shape[1]
    u, q, k, v = jnp.split(h @ w_in, OD_SPLITS, axis=-1)
    if ctx_out:
        uc, qc, kc, vc = jnp.split(hc @ w_in, OD_SPLITS, axis=-1)
    else:
        uc = hc @ w_in[:, :S5_W]
        kc, vc = jnp.split(hc @ w_in[:, OD_K_OFF:], 2, axis=-1)
    y_s5, yc_s5 = s5_bidirectional(u, uc, a_re, a_im, log_dt, b_re, b_im, c_re, c_im, d_skip, glu_w, ctx_out)

    lam = (jnp.exp(jnp.sum(lam_q1.astype(F32) * lam_k1.astype(F32)))
           - jnp.exp(jnp.sum(lam_q2.astype(F32) * lam_k2.astype(F32))) + lam_init)
    q = apply_rope(q.reshape(b, l, DIFF_HEADS, 2, HEAD_DIM), cos, sin)
    k = apply_rope(k.reshape(b, l, DIFF_HEADS, 2, HEAD_DIM), cos, sin)
    v = v.reshape(b, l, DIFF_HEADS, 2 * HEAD_DIM)
    kc = kc.reshape(b, cl, DIFF_HEADS, 2, HEAD_DIM)
    vc = vc.reshape(b, cl, DIFF_HEADS, 2 * HEAD_DIM)
    o = diff_attention(q, jnp.concatenate([kc, k], axis=1), jnp.concatenate([vc, v], axis=1), lam)
    y_att = (rmsnorm(o, subln_g) * (1.0 - lam_init)).reshape(b, l, DIFF_W)
    y = jnp.concatenate([y_s5, y_att], axis=-1) @ w_out
    if not ctx_out:
        return y, None
    oc = diff_attention(qc.reshape(b, cl, DIFF_HEADS, 2, HEAD_DIM), kc, vc, lam)
    yc_att = (rmsnorm(oc, subln_g) * (1.0 - lam_init)).reshape(b, cl, DIFF_W)
    yc = jnp.concatenate([yc_s5, yc_att], axis=-1) @ w_out
    return y, yc


def setup_inputs(seed: int = 0) -> dict:
    key = jax.random.key(seed)
    ks = iter(jax.random.split(key, 40))

    def nrm(shape, std):
        return std * jax.random.normal(next(ks), shape, F32)

    def gain(shape):
        return 1.0 + nrm(shape, 0.02)

    d, ne, no = D_MODEL, N_EVEN, N_ODD
    p = {}
    p['x'] = nrm((BATCH, SEQ, d), 1.0)
    p['c'] = nrm((BATCH, d), 1.0)
    p['ctx'] = nrm((BATCH, CTX_LEN, d), 1.0)
    p['c_ctx'] = nrm((d,), 1.0)
    p['mod_w'] = nrm((DEPTH, d, 6 * d), 0.5 * d ** -0.5)
    p['mod_b'] = nrm((DEPTH, 6 * d), 0.02)
    p['norm_pre'] = gain((DEPTH, 2, d))
    p['norm_post'] = gain((DEPTH, 2, d))
    p['ffn_w_up'] = nrm((DEPTH, d, 2 * D_FF), d ** -0.5)
    p['ffn_conv_w'] = nrm((DEPTH, 3, 2 * D_FF), 3 ** -0.5)
    p['ffn_conv_b'] = nrm((DEPTH, 2 * D_FF), 0.02)
    p['ffn_w_down'] = nrm((DEPTH, D_FF, d), D_FF ** -0.5)
    p['ev_w_in'] = nrm((ne, d, EV_IN), d ** -0.5)
    p['ev_conv_w'] = nrm((ne, 3, CONV_W), 3 ** -0.5)
    p['ev_q_norm'] = gain((ne, HEAD_DIM))
    p['ev_k_norm'] = gain((ne, HEAD_DIM))
    p['ev_w_out'] = nrm((ne, EV_OUT, d), EV_OUT ** -0.5)
    p['od_w_in'] = nrm((no, d, OD_IN), d ** -0.5)
    p['od_s5_a_re'] = -0.5 * jnp.exp(nrm((no, 2, S5_GROUPS, S5_STATE), 0.05))
    p['od_s5_a_im'] = jnp.pi * jnp.arange(S5_STATE, dtype=F32) + nrm((no, 2, S5_GROUPS, S5_STATE), 0.01)
    p['od_s5_log_dt'] = jax.random.uniform(next(ks), (no, 2, S5_GROUPS), F32,
                                           math.log(S5_DT_MIN), math.log(S5_DT_MAX))
    p['od_s5_b_re'] = nrm((no, 2, S5_GROUPS, S5_STATE, S5_GROUP_CH), (2 * S5_GROUP_CH) ** -0.5)
    p['od_s5_b_im'] = nrm((no, 2, S5_GROUPS, S5_STATE, S5_GROUP_CH), (2 * S5_GROUP_CH) ** -0.5)
    p['od_s5_c_re'] = nrm((no, 2, S5_GROUPS, S5_GROUP_CH, S5_STATE), (2 * S5_STATE) ** -0.5)
    p['od_s5_c_im'] = nrm((no, 2, S5_GROUPS, S5_GROUP_CH, S5_STATE), (2 * S5_STATE) ** -0.5)
    p['od_s5_d'] = nrm((no, S5_W), 1.0)
    p['od_glu_w'] = nrm((no, S5_W, 2 * S5_W), S5_W ** -0.5)
    p['od_lam_q1'] = nrm((no, HEAD_DIM), 0.1)
    p['od_lam_k1'] = nrm((no, HEAD_DIM), 0.1)
    p['od_lam_q2'] = nrm((no, HEAD_DIM), 0.1)
    p['od_lam_k2'] = nrm((no, HEAD_DIM), 0.1)
    p['od_subln'] = gain((no, 2 * HEAD_DIM))
    p['od_w_out'] = nrm((no, OD_OUT, d), OD_OUT ** -0.5)
    return p


def reference(x, c, ctx, c_ctx, mod_w, mod_b, norm_pre, norm_post,
              ffn_w_up, ffn_conv_w, ffn_conv_b, ffn_w_down,
              ev_w_in, ev_conv_w, ev_q_norm, ev_k_norm, ev_w_out,
              od_w_in, od_s5_a_re, od_s5_a_im, od_s5_log_dt, od_s5_b_re, od_s5_b_im,
              od_s5_c_re, od_s5_c_im, od_s5_d, od_glu_w,
              od_lam_q1, od_lam_k1, od_lam_q2, od_lam_k2, od_subln, od_w_out):
    b, l, d = x.shape
    ROWS = l // GRID_W
    cos, sin = rope_tables(ROWS)
    xc = ctx
    s_c = jax.nn.silu(c)
    s_cc = jax.nn.silu(c_ctx)
    for i in range(DEPTH):
        ctx_out = i < DEPTH - 1
        j = i // 2
        m = (s_c @ mod_w[i] + mod_b[i]).reshape(b, 6, 1, d)
        mc = (s_cc @ mod_w[i] + mod_b[i]).reshape(6, 1, 1, d)
        h = modulate(x, norm_pre[i, 0], m[:, 0], m[:, 1])
        hc = modulate(xc, norm_pre[i, 0], mc[0], mc[1])
        if i % 2 == 0:
            y, yc = mixer_conv_gqa(h, hc, ev_w_in[j], ev_conv_w[j], ev_q_norm[j], ev_k_norm[j], ev_w_out[j],
                                   cos, sin, ctx_out)
        else:
            y, yc = mixer_s5_diff(h, hc, od_w_in[j], od_s5_a_re[j], od_s5_a_im[j], od_s5_log_dt[j],
                                  od_s5_b_re[j], od_s5_b_im[j], od_s5_c_re[j], od_s5_c_im[j], od_s5_d[j],
                                  od_glu_w[j], od_lam_q1[j], od_lam_k1[j], od_lam_q2[j], od_lam_k2[j],
                                  od_subln[j], od_w_out[j], cos, sin,
                                  0.8 - 0.6 * math.exp(-0.3 * i), ctx_out)
        x = x + m[:, 2] * rmsnorm(y, norm_post[i, 0])
        h = modulate(x, norm_pre[i, 1], m[:, 3], m[:, 4])
        x = x + m[:, 5] * rmsnorm(conv_ffn(h, ffn_w_up[i], ffn_conv_w[i], ffn_conv_b[i], ffn_w_down[i]),
                                  norm_post[i, 1])
        if ctx_out:
            xc = xc + mc[2] * rmsnorm(yc, norm_post[i, 0])
            hc = modulate(xc, norm_pre[i, 1], mc[3], mc[4])
            xc = xc + mc[5] * rmsnorm(conv_ffn(hc, ffn_w_up[i], ffn_conv_w[i], ffn_conv_b[i], ffn_w_down[i]),
                                      norm_post[i, 1])
    return x
```

```python
import functools
import math

import jax
import jax.numpy as jnp
from jax import lax
from jax.experimental import pallas as pl
from jax.experimental.pallas import tpu as pltpu

F32 = jnp.float32
BF16 = jnp.bfloat16

GRID_W = 64
HEAD_DIM = 64
ROPE_THETA = 10000.0
ROPE_FREQS = HEAD_DIM // 4
RMS_EPS = 1e-6
GQA_HEADS = 8
KV_HEADS = 2
DIFF_HEADS = 4
S5_GROUP_CH = 16
S5_STATE = 64
S5_CHUNK = 16
LOG2E = 1.4426950408889634
Q_PRESCALE = HEAD_DIM ** -0.5 * LOG2E
NEG_BIG = -1e30

LANES = 128
VMEM_LIMIT_BYTES = 56 * 1024 * 1024


def _cparams(sem):
    return pltpu.CompilerParams(dimension_semantics=sem, vmem_limit_bytes=VMEM_LIMIT_BYTES)


def _full(shape):
    nd = len(shape)
    return pl.BlockSpec(shape, lambda *_: (0,) * nd)


def _rows(tm, width):
    return pl.BlockSpec((tm, width), lambda i, *_: (i, 0))


def _dot(a, b):
    return jnp.dot(a, b, preferred_element_type=F32)


def _rms_mod(x, g, shift, scale):
    ms = jnp.mean(x * x, axis=-1, keepdims=True)
    return x * lax.rsqrt(ms + RMS_EPS) * (g * (1.0 + scale)) + shift


def _rms(y, g):
    ms = jnp.mean(y * y, axis=-1, keepdims=True)
    return y * lax.rsqrt(ms + RMS_EPS) * g


def _split_dot(a, b_bf16):
    hi = a.astype(BF16)
    lo = (a - hi.astype(F32)).astype(BF16)
    return _dot(hi, b_bf16) + _dot(lo, b_bf16)


def _mod_kernel(s_ref, w_ref, b_ref, o_ref):
    s = s_ref[...]
    s = s * jax.nn.sigmoid(s)
    w = w_ref[0]
    s_hi = s.astype(BF16)
    s_lo = (s - s_hi.astype(F32)).astype(BF16)
    w_hi = w.astype(BF16)
    w_lo = (w - w_hi.astype(F32)).astype(BF16)
    o_ref[0] = _dot(s_hi, w_hi) + _dot(s_lo, w_hi) + _dot(s_hi, w_lo) + b_ref[0]


def _mod_vectors(c, c_ctx, mod_w, mod_b):
    depth, d, n = mod_w.shape
    rows = jnp.zeros((8, d), F32).at[0].set(c[0]).at[1].set(c_ctx)
    tn = 1536
    return pl.pallas_call(
        _mod_kernel,
        grid=(depth, n // tn),
        in_specs=[_full((8, d)),
                  pl.BlockSpec((1, d, tn), lambda i, j: (i, 0, j)),
                  pl.BlockSpec((1, 1, tn), lambda i, j: (i, 0, j))],
        out_specs=pl.BlockSpec((1, 8, tn), lambda i, j: (i, 0, j)),
        out_shape=jax.ShapeDtypeStruct((depth, 8, n), F32),
        compiler_params=_cparams(("arbitrary", "arbitrary")),
        name="mod_vectors",
    )(rows, mod_w, mod_b.reshape(depth, 1, n))


def _tile4(t):
    return jnp.concatenate([t, t, t, t], axis=-1)


def _inproj_even_kernel(x_ref, sh_ref, sc_ref, g_ref, w_ref, bd_ref, cq_ref, sq_ref, ck_ref, sk_ref,
                        gb_ref, gchv_ref, q_ref, k_ref, v_ref):
    h = _rms_mod(x_ref[...], g_ref[...], sh_ref[...], sc_ref[...]).astype(BF16)

    def proj(a, b):
        return _dot(h, w_ref[:, a:b])

    gb_ref[...] = proj(0, 512)
    gchv_ref[...] = proj(512, 1024) * proj(1024, 1536)
    v_ref[...] = proj(2176, 2304).astype(BF16)
    q = proj(1536, 2048)
    qs = proj(2304, 2816)
    k = proj(2048, 2176)
    ks = proj(2816, 2944)
    bd = bd_ref[...]
    rq = lax.rsqrt(_split_dot(q * q, bd) * (1.0 / HEAD_DIM) + RMS_EPS)
    rk = lax.rsqrt(_split_dot(k * k, bd[0:128, 0:128]) * (1.0 / HEAD_DIM) + RMS_EPS)
    q_ref[...] = ((q * _tile4(cq_ref[...]) + qs * _tile4(sq_ref[...])) * (rq * Q_PRESCALE)).astype(BF16)
    k_ref[...] = ((k * ck_ref[...] + ks * sk_ref[...]) * rk).astype(BF16)


def _inproj_even(x, shift, scale, g, w_ext, bd, cq, sq, ck, sk):
    l, d = x.shape
    tm = min(512, l)
    vec = _full((1, d))
    tab = _rows(tm, 128)
    return pl.pallas_call(
        _inproj_even_kernel,
        grid=(l // tm,),
        in_specs=[_rows(tm, d), vec, vec, vec, _full(w_ext.shape), _full(bd.shape), tab, tab, tab, tab],
        out_specs=[_rows(tm, 512), _rows(tm, 512), _rows(tm, 512), _rows(tm, 128), _rows(tm, 128)],
        out_shape=[jax.ShapeDtypeStruct((l, 512), F32), jax.ShapeDtypeStruct((l, 512), F32),
                   jax.ShapeDtypeStruct((l, 512), BF16), jax.ShapeDtypeStruct((l, 128), BF16),
                   jax.ShapeDtypeStruct((l, 128), BF16)],
        compiler_params=_cparams(("parallel",)),
        name="inproj_even",
    )(x, shift, scale, g, w_ext, bd, cq, sq, ck, sk)


def _inproj_odd_kernel(x_ref, sh_ref, sc_ref, g_ref, w_ref, c_ref, s_ref, u_ref, q_ref, k_ref, v_ref):
    h = _rms_mod(x_ref[...], g_ref[...], sh_ref[...], sc_ref[...]).astype(BF16)

    def proj(a, b):
        return _dot(h, w_ref[:, a:b])

    u_ref[...] = proj(0, 512)
    v_ref[...] = proj(1536, 2048).astype(BF16)
    c4 = _tile4(c_ref[...])
    s4 = _tile4(s_ref[...])
    q_ref[...] = ((proj(512, 1024) * c4 + proj(2048, 2560) * s4) * Q_PRESCALE).astype(BF16)
    k_ref[...] = (proj(1024, 1536) * c4 + proj(2560, 3072) * s4).astype(BF16)


def _inproj_odd(x, shift, scale, g, w_ext, ctab, stab):
    l, d = x.shape
    tm = min(512, l)
    vec = _full((1, d))
    tab = _rows(tm, 128)
    return pl.pallas_call(
        _inproj_odd_kernel,
        grid=(l // tm,),
        in_specs=[_rows(tm, d), vec, vec, vec, _full(w_ext.shape), tab, tab],
        out_specs=[_rows(tm, 512)] * 4,
        out_shape=[jax.ShapeDtypeStruct((l, 512), F32)] + [jax.ShapeDtypeStruct((l, 512), BF16)] * 3,
        compiler_params=_cparams(("parallel",)),
        name="inproj_odd",
    )(x, shift, scale, g, w_ext, ctab, stab)


def _pick_tk(s):
    for cand in (1280, 640, 512, 256, 128):
        if s % cand == 0:
            return cand
    raise ValueError(f"no key chunk size divides {s}")


def _flash_sweep(qs_ref, k_at, v_at, m_ref, l_ref, acc_ref, tk, nk):
    m_ref[...] = jnp.full(m_ref.shape, NEG_BIG, F32)
    l_ref[...] = jnp.zeros(l_ref.shape, F32)
    acc_ref[...] = jnp.zeros(acc_ref.shape, F32)

    def body(c, carry):
        off = pl.multiple_of(c * tk, tk)
        s = lax.dot_general(qs_ref[...], k_at(off), (((1,), (1,)), ((), ())),
                            preferred_element_type=F32)
        m_prev = m_ref[...]
        m_new = jnp.maximum(m_prev, jnp.max(s, axis=1, keepdims=True))
        alpha = jnp.exp2(m_prev - m_new)
        p = jnp.exp2(s - m_new[:, 0:1])
        l_ref[...] = alpha * l_ref[...] + jnp.sum(p, axis=1, keepdims=True)
        acc_ref[...] = alpha * acc_ref[...] + _dot(p.astype(BF16), v_at(off))
        m_ref[...] = m_new
        return carry

    lax.fori_loop(0, nk, body, 0)
    return acc_ref[...] / l_ref[...]


def _split_heads(q_blk, qs_ref, row0, tq):
    lo = lax.broadcasted_iota(jnp.int32, q_blk.shape, 1) < HEAD_DIM
    zero = jnp.zeros_like(q_blk)
    qs_ref[row0:row0 + tq, :] = jnp.where(lo, q_blk, zero)
    qs_ref[row0 + tq:row0 + 2 * tq, :] = jnp.where(lo, zero, q_blk)


def _gqa_kernel(q_ref, k_ref, v_ref, o_ref, qs_ref, m_ref, l_ref, acc_ref, *, tq, tk, nk):
    q = q_ref[...]
    for j in range(2):
        _split_heads(q[:, LANES * j:LANES * (j + 1)], qs_ref, 2 * j * tq, tq)
    o = _flash_sweep(qs_ref, lambda off: k_ref[0, pl.ds(off, tk), :], lambda off: v_ref[0, pl.ds(off, tk), :],
                     m_ref, l_ref, acc_ref, tk, nk)
    lo = lax.broadcasted_iota(jnp.int32, (tq, LANES), 1) < HEAD_DIM
    for j in range(2):
        oa = o[2 * j * tq:(2 * j + 1) * tq]
        ob = o[(2 * j + 1) * tq:(2 * j + 2) * tq]
        o_ref[:, LANES * j:LANES * (j + 1)] = jnp.where(lo, oa, ob).astype(BF16)


def _gqa_attention(q, kdup, vdup):
    l = q.shape[0]
    s = kdup.shape[1]
    tq = min(128, l)
    tk = _pick_tk(s)
    r = 4 * tq
    kern = functools.partial(_gqa_kernel, tq=tq, tk=tk, nk=s // tk)
    return pl.pallas_call(
        kern,
        grid=(KV_HEADS, l // tq),
        in_specs=[pl.BlockSpec((tq, 256), lambda kv, i: (i, kv)),
                  pl.BlockSpec((1, s, LANES), lambda kv, i: (kv, 0, 0)),
                  pl.BlockSpec((1, s, LANES), lambda kv, i: (kv, 0, 0))],
        out_specs=pl.BlockSpec((tq, 256), lambda kv, i: (i, kv)),
        out_shape=jax.ShapeDtypeStruct((l, 512), BF16),
        scratch_shapes=[pltpu.VMEM((r, LANES), BF16), pltpu.VMEM((r, LANES), F32),
                        pltpu.VMEM((r, LANES), F32), pltpu.VMEM((r, LANES), F32)],
        compiler_params=_cparams(("parallel", "parallel")),
        name="gqa_attention",
    )(q, kdup, vdup)


def _diff_kernel(q_ref, k_ref, v_ref, lq1_ref, lk1_ref, lq2_ref, lk2_ref, sg_ref, o_ref,
                 qs_ref, m_ref, l_ref, acc_ref, *, tq, tk, nk, lam_init):
    _split_heads(q_ref[...], qs_ref, 0, tq)
    o = _flash_sweep(qs_ref, lambda off: k_ref[pl.ds(off, tk), :], lambda off: v_ref[pl.ds(off, tk), :],
                     m_ref, l_ref, acc_ref, tk, nk)
    lam = (jnp.exp(jnp.sum(lq1_ref[...] * lk1_ref[...], axis=-1, keepdims=True))
           - jnp.exp(jnp.sum(lq2_ref[...] * lk2_ref[...], axis=-1, keepdims=True)) + lam_init)
    d = o[0:tq] - lam * o[tq:2 * tq]
    o_ref[...] = (_rms(d, sg_ref[...]) * (1.0 - lam_init)).astype(BF16)


def _diff_attention(q, k, v, lq1, lk1, lq2, lk2, subln, lam_init):
    l = q.shape[0]
    s = k.shape[0]
    tq = min(256, l)
    tk = _pick_tk(s)
    r = 2 * tq
    kern = functools.partial(_diff_kernel, tq=tq, tk=tk, nk=s // tk, lam_init=lam_init)
    vec = _full((1, HEAD_DIM))
    return pl.pallas_call(
        kern,
        grid=(DIFF_HEADS, l // tq),
        in_specs=[pl.BlockSpec((tq, LANES), lambda h, i: (i, h)),
                  pl.BlockSpec((s, LANES), lambda h, i: (0, h)),
                  pl.BlockSpec((s, LANES), lambda h, i: (0, h)),
                  vec, vec, vec, vec, _full((1, 2 * HEAD_DIM))],
        out_specs=pl.BlockSpec((tq, LANES), lambda h, i: (i, h)),
        out_shape=jax.ShapeDtypeStruct((l, 512), BF16),
        scratch_shapes=[pltpu.VMEM((r, LANES), BF16), pltpu.VMEM((r, LANES), F32),
                        pltpu.VMEM((r, LANES), F32), pltpu.VMEM((r, LANES), F32)],
        compiler_params=_cparams(("parallel", "parallel")),
        name="diff_attention",
    )(q, k, v, lq1, lk1, lq2, lk2, subln)


def _halo_maps(tm, halo, l):
    per = tm // halo
    last = l // halo - 1
    prev = lambda i, *_: (jnp.maximum(i * per - 1, 0), 0)
    nxt = lambda i, *_: (jnp.minimum((i + 1) * per, last), 0)
    return prev, nxt


def _dwconv3_rows(prev, main, nxt, w, halo):
    ext = jnp.concatenate([prev, main, nxt], axis=0)
    n = ext.shape[0]
    tm = main.shape[0]
    up = pltpu.roll(ext, 1, 0)[halo:halo + tm]
    dn = pltpu.roll(ext, n - 1, 0)[halo:halo + tm]
    return up * w[0:1] + main * w[1:2] + dn * w[2:3]


def _outproj_even_kernel(x_ref, gb_ref, gm_ref, gp_ref, gn_ref, cw_ref, o_ref, w_ref, gate_ref, gpost_ref,
                         out_ref, *, nrow):
    i = pl.program_id(0)
    prev = gp_ref[...] * (i > 0).astype(F32)
    nxt = gn_ref[...] * (i < nrow - 1).astype(F32)
    conv = _dwconv3_rows(prev, gm_ref[...], nxt, cw_ref[...], 8)
    yc = (gb_ref[...] * conv).astype(BF16)
    y = _dot(yc, w_ref[0:512, :]) + _dot(o_ref[...], w_ref[512:1024, :])
    out_ref[...] = x_ref[...] + gate_ref[...] * _rms(y, gpost_ref[...])


def _outproj_even(x, gb, gchv, conv_w, o, w_out, gate, gpost):
    l, d = x.shape
    tm = min(512, l)
    nrow = l // tm
    prev, nxt = _halo_maps(tm, 8, l)
    vec = _full((1, d))
    return pl.pallas_call(
        functools.partial(_outproj_even_kernel, nrow=nrow),
        grid=(nrow,),
        in_specs=[_rows(tm, d), _rows(tm, 512), _rows(tm, 512), pl.BlockSpec((8, 512), prev),
                  pl.BlockSpec((8, 512), nxt), _full(conv_w.shape), _rows(tm, 512), _full(w_out.shape), vec, vec],
        out_specs=_rows(tm, d),
        out_shape=jax.ShapeDtypeStruct((l, d), F32),
        compiler_params=_cparams(("parallel",)),
        name="outproj_even",
    )(x, gb, gchv, gchv, gchv, conv_w, o, w_out, gate, gpost)


def _outproj_odd_kernel(x_ref, y_ref, u_ref, dsk_ref, glu_ref, o_ref, w_ref, gate_ref, gpost_ref, out_ref):
    z = jax.nn.gelu(y_ref[...] + dsk_ref[...] * u_ref[...]).astype(BF16)
    zz = _dot(z, glu_ref[...])
    a = (zz[:, 0:512] * jax.nn.sigmoid(zz[:, 512:1024])).astype(BF16)
    y = _dot(a, w_ref[0:512, :]) + _dot(o_ref[...], w_ref[512:1024, :])
    out_ref[...] = x_ref[...] + gate_ref[...] * _rms(y, gpost_ref[...])


def _outproj_odd(x, ys5, u, d_skip, glu_w, o, w_out, gate, gpost):
    l, d = x.shape
    tm = min(512, l)
    vec = _full((1, d))
    return pl.pallas_call(
        _outproj_odd_kernel,
        grid=(l // tm,),
        in_specs=[_rows(tm, d), _rows(tm, 512), _rows(tm, 512), _full((1, 512)), _full(glu_w.shape),
                  _rows(tm, 512), _full(w_out.shape), vec, vec],
        out_specs=_rows(tm, d),
        out_shape=jax.ShapeDtypeStruct((l, d), F32),
        compiler_params=_cparams(("parallel",)),
        name="outproj_odd",
    )(x, ys5, u, d_skip, glu_w, o, w_out, gate, gpost)


FFN_HALO = 16


def _ffn_kernel(xm_ref, xp_ref, xn_ref, sh_ref, sc_ref, g_ref, gate_ref, gpost_ref,
                wa_ref, wg_ref, cwa_ref, cwg_ref, cba_ref, cbg_ref, wd_ref, out_ref,
                hext_ref, acc_ref, *, tm, nrow, nchunk):
    i = pl.program_id(0)
    c = pl.program_id(1)

    @pl.when(c == 0)
    def _():
        g, sh, sc = g_ref[...], sh_ref[...], sc_ref[...]
        hp = _rms_mod(xp_ref[...], g, sh, sc) * (i > 0).astype(F32)
        hn = _rms_mod(xn_ref[...], g, sh, sc) * (i < nrow - 1).astype(F32)
        hext_ref[0:FFN_HALO, :] = hp.astype(BF16)
        hext_ref[FFN_HALO:FFN_HALO + tm, :] = _rms_mod(xm_ref[...], g, sh, sc).astype(BF16)
        hext_ref[FFN_HALO + tm:2 * FFN_HALO + tm, :] = hn.astype(BF16)
        acc_ref[...] = jnp.zeros(acc_ref.shape, F32)

    he = hext_ref[...]
    n = tm + 2 * FFN_HALO

    def branch(w_ref, cw_ref, cb_ref):
        p = _dot(he, w_ref[...])
        cw = cw_ref[...]
        up = pltpu.roll(p, 1, 0)[FFN_HALO:FFN_HALO + tm]
        dn = pltpu.roll(p, n - 1, 0)[FFN_HALO:FFN_HALO + tm]
        return up * cw[0:1] + p[FFN_HALO:FFN_HALO + tm] * cw[1:2] + dn * cw[2:3] + cb_ref[...]

    ua = branch(wa_ref, cwa_ref, cba_ref)
    ug = branch(wg_ref, cwg_ref, cbg_ref)
    act = (ua * (ug * jax.nn.sigmoid(ug))).astype(BF16)
    acc_ref[...] += _dot(act, wd_ref[...])

    @pl.when(c == nchunk - 1)
    def _():
        out_ref[...] = xm_ref[...] + gate_ref[...] * _rms(acc_ref[...], gpost_ref[...])


def _conv_ffn(x, shift, scale, g, gate, gpost, w_up, conv_w, conv_b, w_down):
    l, d = x.shape
    dff = w_down.shape[0]
    tm = min(512, l)
    tn = 256
    nrow = l // tm
    nchunk = dff // tn
    prev, nxt = _halo_maps(tm, FFN_HALO, l)
    vec = _full((1, d))
    col_a = lambda i, c: (0, c)
    col_g = lambda i, c: (0, c + nchunk)
    return pl.pallas_call(
        functools.partial(_ffn_kernel, tm=tm, nrow=nrow, nchunk=nchunk),
        grid=(nrow, nchunk),
        in_specs=[_rows(tm, d), pl.BlockSpec((FFN_HALO, d), prev), pl.BlockSpec((FFN_HALO, d), nxt),
                  vec, vec, vec, vec, vec,
                  pl.BlockSpec((d, tn), col_a), pl.BlockSpec((d, tn), col_g),
                  pl.BlockSpec((3, tn), col_a), pl.BlockSpec((3, tn), col_g),
                  pl.BlockSpec((1, tn), col_a), pl.BlockSpec((1, tn), col_g),
                  pl.BlockSpec((tn, d), lambda i, c: (c, 0))],
        out_specs=_rows(tm, d),
        out_shape=jax.ShapeDtypeStruct((l, d), F32),
        scratch_shapes=[pltpu.VMEM((tm + 2 * FFN_HALO, d), BF16), pltpu.VMEM((tm, d), F32)],
        compiler_params=_cparams(("parallel", "arbitrary")),
        name="conv_ffn",
    )(x, x, x, shift, scale, g, gate, gpost, w_up, w_up, conv_w, conv_w, conv_b, conv_b, w_down)


S5_GB = 4


def _s5_kernel(zl_ref, zc_ref, w1_ref, wo_ref, a1_ref, a2_ref, y_ref, yi_ref, s_ref, ss_ref, hp_ref,
               *, nlat, nctx):
    nch = nlat + nctx
    gw = 2 * S5_STATE
    for d in range(2):
        lat0 = nctx if d == 0 else 0
        ctx0 = 0 if d == 0 else nlat
        for g in range(S5_GB):
            for z_ref, r0, nr in ((zl_ref, lat0, nlat), (zc_ref, ctx0, nctx)):
                r = _dot(z_ref[g], w1_ref[d, g])
                yi_ref[g, r0:r0 + nr, :] = r[:, 0:256]
                s_ref[r0:r0 + nr, g * gw:(g + 1) * gw] = r[:, 256:384]
                ss_ref[r0:r0 + nr, g * gw:(g + 1) * gw] = r[:, 384:512]
        a1 = a1_ref[d]
        a2 = a2_ref[d]

        def step(t, carry, d=d, a1=a1, a2=a2):
            h, hs = carry
            j = t if d == 0 else nch - 1 - t
            hp_ref[pl.ds(j, 1), :] = h
            s = s_ref[pl.ds(j, 1), :]
            ss = ss_ref[pl.ds(j, 1), :]
            return h * a1 + hs * a2 + s, hs * a1 - h * a2 + ss

        zero = jnp.zeros((1, S5_GB * gw), F32)
        lax.fori_loop(0, nch, step, (zero, zero))
        for g in range(S5_GB):
            hprev = hp_ref[lat0:lat0 + nlat, g * gw:(g + 1) * gw].astype(BF16)
            y = yi_ref[g, lat0:lat0 + nlat, :] + _dot(hprev, wo_ref[d, g])
            if d == 0:
                y_ref[g] = y
            else:
                y_ref[g] += y


def _s5_weights(a_re, a_im, log_dt, b_re, b_im, c_re, c_im):
    t = S5_CHUNK
    hp = lax.Precision.HIGHEST
    dt = jnp.exp(log_dt)[..., None]
    mag = jnp.exp(a_re * dt)
    ab_re = mag * jnp.cos(a_im * dt)
    ab_im = mag * jnp.sin(a_im * dt)
    nr, ni = ab_re - 1.0, ab_im
    den = a_re * a_re + a_im * a_im
    cr = ((nr * a_re + ni * a_im) / den)[..., None]
    ci = ((ni * a_re - nr * a_im) / den)[..., None]
    bb_re = cr * b_re - ci * b_im
    bb_im = cr * b_im + ci * b_re
    k = jnp.arange(t + 1, dtype=F32)[:, None, None, None]
    pmag = jnp.exp(k * (a_re * dt))
    pw_re = pmag * jnp.cos(k * (a_im * dt))
    pw_im = pmag * jnp.sin(k * (a_im * dt))
    ab_k_re = pw_re[..., None] * bb_re - pw_im[..., None] * bb_im
    ab_k_im = pw_re[..., None] * bb_im + pw_im[..., None] * bb_re
    kern = (jnp.einsum('dgcp,kdgph->kdgch', c_re, ab_k_re, precision=hp)
            - jnp.einsum('dgcp,kdgph->kdgch', c_im, ab_k_im, precision=hp))
    ca_re = c_re[None] * pw_re[:, :, :, None, :] - c_im[None] * pw_im[:, :, :, None, :]
    ca_im = c_re[None] * pw_im[:, :, :, None, :] + c_im[None] * pw_re[:, :, :, None, :]
    g = a_re.shape[1]
    sig = jnp.arange(t)[:, None]
    tau = jnp.arange(t)[None, :]
    w1, wo, a1, a2 = [], [], [], []
    for d in range(2):
        lag = (tau - sig) if d == 0 else (sig - tau)
        m = kern[jnp.clip(lag, 0, t), d] * (lag >= 0)[:, :, None, None, None]
        m = m.transpose(2, 0, 4, 1, 3).reshape(g, t * S5_GROUP_CH, t * S5_GROUP_CH)
        pin = (t - 1 - jnp.arange(t)) if d == 0 else jnp.arange(t)
        st_re = ab_k_re[pin, d].transpose(1, 0, 3, 2).reshape(g, t * S5_GROUP_CH, S5_STATE)
        st_im = ab_k_im[pin, d].transpose(1, 0, 3, 2).reshape(g, t * S5_GROUP_CH, S5_STATE)
        w1.append(jnp.concatenate([m, st_re, st_im, st_im, st_re], axis=-1))
        pout = (jnp.arange(t) + 1) if d == 0 else (t - jnp.arange(t))
        o_re = ca_re[pout, d].transpose(1, 3, 0, 2).reshape(g, S5_STATE, t * S5_GROUP_CH)
        o_im = ca_im[pout, d].transpose(1, 3, 0, 2).reshape(g, S5_STATE, t * S5_GROUP_CH)
        wo.append(jnp.concatenate([o_re, -o_im], axis=1))
        ar, ai = pw_re[t, d], pw_im[t, d]
        a1.append(jnp.concatenate([ar, ar], axis=-1).reshape(1, -1))
        a2.append(jnp.concatenate([-ai, ai], axis=-1).reshape(1, -1))
    return (jnp.stack(w1).astype(BF16), jnp.stack(wo).astype(BF16), jnp.stack(a1), jnp.stack(a2))


def _to_group_major(u):
    l, w = u.shape
    g = w // S5_GROUP_CH
    return (u.reshape(l // S5_CHUNK, S5_CHUNK, g, S5_GROUP_CH).transpose(2, 0, 1, 3)
            .reshape(g, l // S5_CHUNK, S5_CHUNK * S5_GROUP_CH))


def _s5_mix(u, uc, a_re, a_im, log_dt, b_re, b_im, c_re, c_im):
    l, w = u.shape
    g = w // S5_GROUP_CH
    w1, wo, a1, a2 = _s5_weights(a_re, a_im, log_dt, b_re, b_im, c_re, c_im)
    zl = _to_group_major(u).astype(BF16)
    zc = _to_group_major(uc).astype(BF16)
    nlat, nctx = zl.shape[1], zc.shape[1]
    nch = nlat + nctx
    cw = S5_CHUNK * S5_GROUP_CH
    gw = 2 * S5_STATE
    y = pl.pallas_call(
        functools.partial(_s5_kernel, nlat=nlat, nctx=nctx),
        grid=(g // S5_GB,),
        in_specs=[pl.BlockSpec((S5_GB, nlat, cw), lambda i: (i, 0, 0)),
                  pl.BlockSpec((S5_GB, nctx, cw), lambda i: (i, 0, 0)),
                  pl.BlockSpec((2, S5_GB, cw, cw + 2 * gw), lambda i: (0, i, 0, 0)),
                  pl.BlockSpec((2, S5_GB, gw, cw), lambda i: (0, i, 0, 0)),
                  pl.BlockSpec((2, 1, S5_GB * gw), lambda i: (0, 0, i)),
                  pl.BlockSpec((2, 1, S5_GB * gw), lambda i: (0, 0, i))],
        out_specs=pl.BlockSpec((S5_GB, nlat, cw), lambda i: (i, 0, 0)),
        out_shape=jax.ShapeDtypeStruct((g, nlat, cw), F32),
        scratch_shapes=[pltpu.VMEM((S5_GB, nch, cw), F32), pltpu.VMEM((nch, S5_GB * gw), F32),
                        pltpu.VMEM((nch, S5_GB * gw), F32), pltpu.VMEM((nch, S5_GB * gw), F32)],
        compiler_params=_cparams(("parallel",)),
        name="s5_mix",
    )(zl, zc, w1, wo, a1, a2)
    return (y.reshape(g, nlat, S5_CHUNK, S5_GROUP_CH).transpose(1, 2, 0, 3).reshape(l, w))


def _swap_pairs(w):
    n = w.shape[-1]
    return w.reshape(w.shape[:-1] + (n // 2, 2))[..., ::-1].reshape(w.shape)


def _rope_tables(rows):
    row = jnp.repeat(jnp.arange(rows, dtype=F32), GRID_W)
    col = jnp.tile(jnp.arange(GRID_W, dtype=F32), rows)
    inv = ROPE_THETA ** (-jnp.arange(ROPE_FREQS, dtype=F32) / ROPE_FREQS)
    ang = jnp.concatenate([row[:, None] * inv, col[:, None] * inv], axis=-1)
    return jnp.cos(ang), jnp.sin(ang)


def _rotary_pair_tables(cos, sin, gain):
    cos_rep = jnp.repeat(cos, 2, axis=-1)
    sin_rep = jnp.repeat(sin, 2, axis=-1)
    sign = jnp.tile(jnp.array([-1.0, 1.0], F32), HEAD_DIM // 2)
    c = cos_rep * gain
    s = sin_rep * sign * _swap_pairs(gain)
    return jnp.tile(c, (1, 2)), jnp.tile(s, (1, 2))


def _block_diag_ones(n, blk):
    idx = jnp.arange(n) // blk
    return (idx[:, None] == idx[None, :]).astype(BF16)


def _dup_heads(k, nheads):
    s = k.shape[0]
    kh = k.reshape(s, nheads, HEAD_DIM).transpose(1, 0, 2)
    return jnp.concatenate([kh, kh], axis=-1)


def kernel(x, c, ctx, c_ctx, mod_w, mod_b, norm_pre, norm_post, ffn_w_up, ffn_conv_w, ffn_conv_b, ffn_w_down,
           ev_w_in, ev_conv_w, ev_q_norm, ev_k_norm, ev_w_out,
           od_w_in, od_s5_a_re, od_s5_a_im, od_s5_log_dt, od_s5_b_re, od_s5_b_im,
           od_s5_c_re, od_s5_c_im, od_s5_d, od_glu_w,
           od_lam_q1, od_lam_k1, od_lam_q2, od_lam_k2, od_subln, od_w_out):
    b, l, d = x.shape
    assert b == 1, "single-sequence kernel"
    depth = mod_w.shape[0]
    cl = ctx.shape[1]
    xs, xc = x[0], ctx[0]
    cos, sin = _rope_tables(l // GRID_W)
    ones_c, zeros_c = jnp.ones((cl, HEAD_DIM // 2), F32), jnp.zeros((cl, HEAD_DIM // 2), F32)
    bd = _block_diag_ones(512, HEAD_DIM)
    mods = _mod_vectors(c, c_ctx, mod_w, mod_b)

    def vecs(i, row):
        m = mods[i, row].reshape(6, 1, d)
        return [m[j] for j in range(6)]

    def ffn(xx, i, m):
        return _conv_ffn(xx, m[3], m[4], norm_pre[i, 1][None], m[5], norm_post[i, 1][None],
                         ffn_w_up[i].astype(BF16), ffn_conv_w[i], ffn_conv_b[i][None], ffn_w_down[i].astype(BF16))

    for i in range(depth):
        ctx_out = i < depth - 1
        j = i // 2
        m, mc = vecs(i, 0), vecs(i, 1)
        g_pre, g_post = norm_pre[i, 0][None], norm_post[i, 0][None]
        if i % 2 == 0:
            w = ev_w_in[j]
            w_ext = jnp.concatenate([w, _swap_pairs(w[:, 1536:2048]), _swap_pairs(w[:, 2048:2176])],
                                    axis=1).astype(BF16)
            w_out = ev_w_out[j].astype(BF16)
            one = jnp.ones((HEAD_DIM,), F32)
            cq, sq = _rotary_pair_tables(cos, sin, ev_q_norm[j])
            ck, sk = _rotary_pair_tables(cos, sin, ev_k_norm[j])
            ccq, csq = _rotary_pair_tables(ones_c, zeros_c, ev_q_norm[j])
            cck, csk = _rotary_pair_tables(ones_c, zeros_c, ev_k_norm[j])
            gb, gchv, q, k, v = _inproj_even(xs, m[0], m[1], g_pre, w_ext, bd, cq, sq, ck, sk)
            cgb, cgchv, qc, kc, vc = _inproj_even(xc, mc[0], mc[1], g_pre, w_ext, bd, ccq, csq, cck, csk)
            kall = _dup_heads(jnp.concatenate([kc, k], axis=0), KV_HEADS)
            vall = _dup_heads(jnp.concatenate([vc, v], axis=0), KV_HEADS)
            o = _gqa_attention(q, kall, vall)
            xs_new = _outproj_even(xs, gb, gchv, ev_conv_w[j], o, w_out, m[2], g_post)
            if ctx_out:
                oc = _gqa_attention(qc, _dup_heads(kc, KV_HEADS), _dup_heads(vc, KV_HEADS))
                xc = _outproj_even(xc, cgb, cgchv, ev_conv_w[j], oc, w_out, mc[2], g_post)
            xs = xs_new
        else:
            w = od_w_in[j]
            w_ext = jnp.concatenate([w, _swap_pairs(w[:, 512:1024]), _swap_pairs(w[:, 1024:1536])],
                                    axis=1).astype(BF16)
            one = jnp.ones((HEAD_DIM,), F32)
            ct, st = _rotary_pair_tables(cos, sin, one)
            cct, cst = _rotary_pair_tables(ones_c, zeros_c, one)
            u, q, k, v = _inproj_odd(xs, m[0], m[1], g_pre, w_ext, ct, st)
            uc, qc, kc, vc = _inproj_odd(xc, mc[0], mc[1], g_pre, w_ext, cct, cst)
            ys5 = _s5_mix(u, uc, od_s5_a_re[j], od_s5_a_im[j], od_s5_log_dt[j], od_s5_b_re[j], od_s5_b_im[j],
                          od_s5_c_re[j], od_s5_c_im[j])
            lam_init = 0.8 - 0.6 * math.exp(-0.3 * i)
            o = _diff_attention(q, jnp.concatenate([kc, k], axis=0), jnp.concatenate([vc, v], axis=0),
                                od_lam_q1[j][None], od_lam_k1[j][None], od_lam_q2[j][None], od_lam_k2[j][None],
                                od_subln[j][None], lam_init)
            if ctx_out:
                raise NotImplementedError("context stream after an S5 | diff-attention layer")
            xs = _outproj_odd(xs, ys5, u, od_s5_d[j][None], od_glu_w[j].astype(BF16), o,
                              od_w_out[j].astype(BF16), m[2], g_post)
        xs = ffn(xs, i, m)
        if ctx_out:
            xc = ffn(xc, i, mc)
    return xs[None]
```

```python
import functools
import math

import jax
import jax.numpy as jnp
from jax import lax
from jax.experimental import pallas as pl
from jax.experimental.pallas import tpu as pltpu

F32 = jnp.float32
BF16 = jnp.bfloat16

GRID_W = 64
HEAD_DIM = 64
ROPE_THETA = 10000.0
ROPE_FREQS = HEAD_DIM // 4
RMS_EPS = 1e-6
GQA_HEADS = 8
KV_HEADS = 2
DIFF_HEADS = 4
S5_GROUP_CH = 16
S5_STATE = 64
S5_CHUNK = 16
LOG2E = 1.4426950408889634
Q_PRESCALE = HEAD_DIM ** -0.5 * LOG2E
NEG_BIG = -1e30

LANES = 128
VMEM_LIMIT_BYTES = 56 * 1024 * 1024


def _cparams(sem):
    return pltpu.CompilerParams(dimension_semantics=sem, vmem_limit_bytes=VMEM_LIMIT_BYTES)


def _full(shape):
    nd = len(shape)
    return pl.BlockSpec(shape, lambda *_: (0,) * nd)


def _rows(tm, width):
    return pl.BlockSpec((tm, width), lambda i, *_: (i, 0))


def _dot(a, b):
    return jnp.dot(a, b, preferred_element_type=F32)


def _rms_mod(x, g, shift, scale):
    ms = jnp.mean(x * x, axis=-1, keepdims=True)
    return x * lax.rsqrt(ms + RMS_EPS) * (g * (1.0 + scale)) + shift


def _rms(y, g):
    ms = jnp.mean(y * y, axis=-1, keepdims=True)
    return y * lax.rsqrt(ms + RMS_EPS) * g


def _split_dot(a, b_bf16):
    hi = a.astype(BF16)
    lo = (a - hi.astype(F32)).astype(BF16)
    return _dot(hi, b_bf16) + _dot(lo, b_bf16)


def _mod_kernel(s_ref, w_ref, b_ref, o_ref):
    s = s_ref[...]
    s = s * jax.nn.sigmoid(s)
    w = w_ref[0]
    s_hi = s.astype(BF16)
    s_lo = (s - s_hi.astype(F32)).astype(BF16)
    w_hi = w.astype(BF16)
    w_lo = (w - w_hi.astype(F32)).astype(BF16)
    o_ref[0] = _dot(s_hi, w_hi) + _dot(s_lo, w_hi) + _dot(s_hi, w_lo) + b_ref[0]


def _mod_vectors(c, c_ctx, mod_w, mod_b):
    depth, d, n = mod_w.shape
    rows = jnp.zeros((8, d), F32).at[0].set(c[0]).at[1].set(c_ctx)
    tn = 1536
    return pl.pallas_call(
        _mod_kernel,
        grid=(depth, n // tn),
        in_specs=[_full((8, d)),
                  pl.BlockSpec((1, d, tn), lambda i, j: (i, 0, j)),
                  pl.BlockSpec((1, 1, tn), lambda i, j: (i, 0, j))],
        out_specs=pl.BlockSpec((1, 8, tn), lambda i, j: (i, 0, j)),
        out_shape=jax.ShapeDtypeStruct((depth, 8, n), F32),
        compiler_params=_cparams(("arbitrary", "arbitrary")),
        name="mod_vectors",
    )(rows, mod_w, mod_b.reshape(depth, 1, n))


def _tile4(t):
    return jnp.concatenate([t, t, t, t], axis=-1)


def _inproj_even_kernel(x_ref, sh_ref, sc_ref, g_ref, w_ref, bd_ref, cq_ref, sq_ref, ck_ref, sk_ref,
                        gb_ref, gchv_ref, q_ref, k_ref, v_ref):
    h = _rms_mod(x_ref[...], g_ref[...], sh_ref[...], sc_ref[...]).astype(BF16)

    def proj(a, b):
        return _dot(h, w_ref[:, a:b])

    gb_ref[...] = proj(0, 512)
    gchv_ref[...] = proj(512, 1024) * proj(1024, 1536)
    v_ref[...] = proj(2176, 2304).astype(BF16)
    q = proj(1536, 2048)
    qs = proj(2304, 2816)
    k = proj(2048, 2176)
    ks = proj(2816, 2944)
    bd = bd_ref[...]
    rq = lax.rsqrt(_split_dot(q * q, bd) * (1.0 / HEAD_DIM) + RMS_EPS)
    rk = lax.rsqrt(_split_dot(k * k, bd[0:128, 0:128]) * (1.0 / HEAD_DIM) + RMS_EPS)
    q_ref[...] = ((q * _tile4(cq_ref[...]) + qs * _tile4(sq_ref[...])) * (rq * Q_PRESCALE)).astype(BF16)
    k_ref[...] = ((k * ck_ref[...] + ks * sk_ref[...]) * rk).astype(BF16)


def _inproj_even(x, shift, scale, g, w_ext, bd, cq, sq, ck, sk):
    l, d = x.shape
    tm = min(512, l)
    vec = _full((1, d))
    tab = _rows(tm, 128)
    return pl.pallas_call(
        _inproj_even_kernel,
        grid=(l // tm,),
        in_specs=[_rows(tm, d), vec, vec, vec, _full(w_ext.shape), _full(bd.shape), tab, tab, tab, tab],
        out_specs=[_rows(tm, 512), _rows(tm, 512), _rows(tm, 512), _rows(tm, 128), _rows(tm, 128)],
        out_shape=[jax.ShapeDtypeStruct((l, 512), F32), jax.ShapeDtypeStruct((l, 512), F32),
                   jax.ShapeDtypeStruct((l, 512), BF16), jax.ShapeDtypeStruct((l, 128), BF16),
                   jax.ShapeDtypeStruct((l, 128), BF16)],
        compiler_params=_cparams(("parallel",)),
        name="inproj_even",
    )(x, shift, scale, g, w_ext, bd, cq, sq, ck, sk)


def _inproj_odd_kernel(x_ref, sh_ref, sc_ref, g_ref, w_ref, c_ref, s_ref, u_ref, q_ref, k_ref, v_ref):
    h = _rms_mod(x_ref[...], g_ref[...], sh_ref[...], sc_ref[...]).astype(BF16)

    def proj(a, b):
        return _dot(h, w_ref[:, a:b])

    u_ref[...] = proj(0, 512)
    v_ref[...] = proj(1536, 2048).astype(BF16)
    c4 = _tile4(c_ref[...])
    s4 = _tile4(s_ref[...])
    q_ref[...] = ((proj(512, 1024) * c4 + proj(2048, 2560) * s4) * Q_PRESCALE).astype(BF16)
    k_ref[...] = (proj(1024, 1536) * c4 + proj(2560, 3072) * s4).astype(BF16)


def _inproj_odd(x, shift, scale, g, w_ext, ctab, stab):
    l, d = x.shape
    tm = min(512, l)
    vec = _full((1, d))
    tab = _rows(tm, 128)
    return pl.pallas_call(
        _inproj_odd_kernel,
        grid=(l // tm,),
        in_specs=[_rows(tm, d), vec, vec, vec, _full(w_ext.shape), tab, tab],
        out_specs=[_rows(tm, 512)] * 4,
        out_shape=[jax.ShapeDtypeStruct((l, 512), F32)] + [jax.ShapeDtypeStruct((l, 512), BF16)] * 3,
        compiler_params=_cparams(("parallel",)),
        name="inproj_odd",
    )(x, shift, scale, g, w_ext, ctab, stab)


def _pick_tk(s):
    for cand in (1280, 640, 512, 256, 128):
        if s % cand == 0:
            return cand
    raise ValueError(f"no key chunk size divides {s}")


def _flash_sweep(qs_ref, k_at, v_at, s_refs, m_ref, acc_ref, tk, nk):
    m_ref[...] = jnp.full(m_ref.shape, NEG_BIG, F32)
    acc_ref[...] = jnp.zeros(acc_ref.shape, F32)
    reps = acc_ref.shape[1] // LANES

    def scores(c, dst):
        off = pl.multiple_of(c * tk, tk)
        dst[...] = lax.dot_general(qs_ref[...], k_at(off), (((1,), (1,)), ((), ())),
                                   preferred_element_type=F32)

    def consume(c, src):
        off = pl.multiple_of(c * tk, tk)
        s = src[...]
        m_prev = m_ref[...]
        m_new = jnp.maximum(m_prev, jnp.max(s, axis=1, keepdims=True))
        alpha = jnp.exp2(m_prev - m_new)
        p = jnp.exp2(s - m_new[:, 0:1]).astype(BF16)
        alpha_w = alpha if reps == 1 else jnp.concatenate([alpha] * reps, axis=1)
        acc_ref[...] = alpha_w * acc_ref[...] + _dot(p, v_at(off))
        m_ref[...] = m_new

    s0, s1 = s_refs
    scores(0, s0)
    npairs = (nk - 1) // 2

    def pair(i, carry):
        c = 2 * i
        scores(c + 1, s1)
        consume(c, s0)
        scores(c + 2, s0)
        consume(c + 1, s1)
        return carry

    lax.fori_loop(0, npairs, pair, 0)
    if nk % 2 == 1:
        consume(nk - 1, s0)
    else:
        scores(nk - 1, s1)
        consume(nk - 2, s0)
        consume(nk - 1, s1)
    return acc_ref[...]


def _split_heads(q_blk, qs_ref, row0, tq):
    lo = lax.broadcasted_iota(jnp.int32, q_blk.shape, 1) < HEAD_DIM
    zero = jnp.zeros_like(q_blk)
    qs_ref[row0:row0 + tq, :] = jnp.where(lo, q_blk, zero)
    qs_ref[row0 + tq:row0 + 2 * tq, :] = jnp.where(lo, zero, q_blk)


def _gqa_kernel(q_ref, k_ref, v_ref, o_ref, qs_ref, s0_ref, s1_ref, m_ref, acc_ref, *, tq, tk, nk):
    q = q_ref[...]
    for j in range(2):
        _split_heads(q[:, LANES * j:LANES * (j + 1)], qs_ref, 2 * j * tq, tq)
    acc = _flash_sweep(qs_ref, lambda off: k_ref[0, pl.ds(off, tk), :], lambda off: v_ref[0, pl.ds(off, tk), :],
                       (s0_ref, s1_ref), m_ref, acc_ref, tk, nk)
    lo = lax.broadcasted_iota(jnp.int32, acc.shape, 1) < HEAD_DIM
    o = acc / jnp.where(lo, pltpu.roll(acc, HEAD_DIM, 1), 1.0)
    lo = lax.broadcasted_iota(jnp.int32, (tq, LANES), 1) < HEAD_DIM
    for j in range(2):
        oa = o[2 * j * tq:(2 * j + 1) * tq]
        ob = pltpu.roll(o[(2 * j + 1) * tq:(2 * j + 2) * tq], HEAD_DIM, 1)
        o_ref[:, LANES * j:LANES * (j + 1)] = jnp.where(lo, oa, ob).astype(BF16)


def _gqa_attention(q, kdup, vext):
    l = q.shape[0]
    s = kdup.shape[1]
    tq = min(128, l)
    tk = _pick_tk(s)
    r = 4 * tq
    kern = functools.partial(_gqa_kernel, tq=tq, tk=tk, nk=s // tk)
    return pl.pallas_call(
        kern,
        grid=(KV_HEADS, l // tq),
        in_specs=[pl.BlockSpec((tq, 256), lambda kv, i: (i, kv)),
                  pl.BlockSpec((1, s, LANES), lambda kv, i: (kv, 0, 0)),
                  pl.BlockSpec((1, s, LANES), lambda kv, i: (kv, 0, 0))],
        out_specs=pl.BlockSpec((tq, 256), lambda kv, i: (i, kv)),
        out_shape=jax.ShapeDtypeStruct((l, 512), BF16),
        scratch_shapes=[pltpu.VMEM((r, LANES), BF16), pltpu.VMEM((r, tk), F32), pltpu.VMEM((r, tk), F32),
                        pltpu.VMEM((r, LANES), F32), pltpu.VMEM((r, LANES), F32)],
        compiler_params=_cparams(("parallel", "parallel")),
        name="gqa_attention",
    )(q, kdup, vext)


def _diff_kernel(q_ref, k_ref, v_ref, lq1_ref, lk1_ref, lq2_ref, lk2_ref, sg_ref, o_ref,
                 qs_ref, s0_ref, s1_ref, m_ref, acc_ref, *, tq, tk, nk, lam_init):
    _split_heads(q_ref[...], qs_ref, 0, tq)
    acc = _flash_sweep(qs_ref, lambda off: k_ref[pl.ds(off, tk), :], lambda off: v_ref[pl.ds(off, tk), :],
                       (s0_ref, s1_ref), m_ref, acc_ref, tk, nk)
    o = acc[:, 0:LANES] / acc[:, LANES:2 * LANES]
    lam = (jnp.exp(jnp.sum(lq1_ref[...] * lk1_ref[...], axis=-1, keepdims=True))
           - jnp.exp(jnp.sum(lq2_ref[...] * lk2_ref[...], axis=-1, keepdims=True)) + lam_init)
    d = o[0:tq] - lam * o[tq:2 * tq]
    o_ref[...] = (_rms(d, sg_ref[...]) * (1.0 - lam_init)).astype(BF16)


def _diff_attention(q, k, vext, lq1, lk1, lq2, lk2, subln, lam_init):
    l = q.shape[0]
    s = k.shape[0]
    tq = min(256, l)
    tk = _pick_tk(s)
    r = 2 * tq
    kern = functools.partial(_diff_kernel, tq=tq, tk=tk, nk=s // tk, lam_init=lam_init)
    vec = _full((1, HEAD_DIM))
    return pl.pallas_call(
        kern,
        grid=(DIFF_HEADS, l // tq),
        in_specs=[pl.BlockSpec((tq, LANES), lambda h, i: (i, h)),
                  pl.BlockSpec((s, LANES), lambda h, i: (0, h)),
                  pl.BlockSpec((s, 2 * LANES), lambda h, i: (0, h)),
                  vec, vec, vec, vec, _full((1, 2 * HEAD_DIM))],
        out_specs=pl.BlockSpec((tq, LANES), lambda h, i: (i, h)),
        out_shape=jax.ShapeDtypeStruct((l, 512), BF16),
        scratch_shapes=[pltpu.VMEM((r, LANES), BF16), pltpu.VMEM((r, tk), F32), pltpu.VMEM((r, tk), F32),
                        pltpu.VMEM((r, LANES), F32), pltpu.VMEM((r, 2 * LANES), F32)],
        compiler_params=_cparams(("parallel", "parallel")),
        name="diff_attention",
    )(q, k, vext, lq1, lk1, lq2, lk2, subln)


def _halo_maps(tm, halo, l):
    per = tm // halo
    last = l // halo - 1
    prev = lambda i, *_: (jnp.maximum(i * per - 1, 0), 0)
    nxt = lambda i, *_: (jnp.minimum((i + 1) * per, last), 0)
    return prev, nxt


def _dwconv3_rows(prev, main, nxt, w, halo):
    ext = jnp.concatenate([prev, main, nxt], axis=0)
    n = ext.shape[0]
    tm = main.shape[0]
    up = pltpu.roll(ext, 1, 0)[halo:halo + tm]
    dn = pltpu.roll(ext, n - 1, 0)[halo:halo + tm]
    return up * w[0:1] + main * w[1:2] + dn * w[2:3]


def _outproj_even_kernel(x_ref, gb_ref, gm_ref, gp_ref, gn_ref, cw_ref, o_ref, w_ref, gate_ref, gpost_ref,
                         out_ref, *, nrow):
    i = pl.program_id(0)
    prev = gp_ref[...] * (i > 0).astype(F32)
    nxt = gn_ref[...] * (i < nrow - 1).astype(F32)
    conv = _dwconv3_rows(prev, gm_ref[...], nxt, cw_ref[...], 8)
    yc = (gb_ref[...] * conv).astype(BF16)
    y = _dot(yc, w_ref[0:512, :]) + _dot(o_ref[...], w_ref[512:1024, :])
    out_ref[...] = x_ref[...] + gate_ref[...] * _rms(y, gpost_ref[...])


def _outproj_even(x, gb, gchv, conv_w, o, w_out, gate, gpost):
    l, d = x.shape
    tm = min(512, l)
    nrow = l // tm
    prev, nxt = _halo_maps(tm, 8, l)
    vec = _full((1, d))
    return pl.pallas_call(
        functools.partial(_outproj_even_kernel, nrow=nrow),
        grid=(nrow,),
        in_specs=[_rows(tm, d), _rows(tm, 512), _rows(tm, 512), pl.BlockSpec((8, 512), prev),
                  pl.BlockSpec((8, 512), nxt), _full(conv_w.shape), _rows(tm, 512), _full(w_out.shape), vec, vec],
        out_specs=_rows(tm, d),
        out_shape=jax.ShapeDtypeStruct((l, d), F32),
        compiler_params=_cparams(("parallel",)),
        name="outproj_even",
    )(x, gb, gchv, gchv, gchv, conv_w, o, w_out, gate, gpost)


def _outproj_odd_kernel(x_ref, y_ref, u_ref, dsk_ref, glu_ref, o_ref, w_ref, gate_ref, gpost_ref, out_ref):
    z = jax.nn.gelu(y_ref[...] + dsk_ref[...] * u_ref[...]).astype(BF16)
    zz = _dot(z, glu_ref[...])
    a = (zz[:, 0:512] * jax.nn.sigmoid(zz[:, 512:1024])).astype(BF16)
    y = _dot(a, w_ref[0:512, :]) + _dot(o_ref[...], w_ref[512:1024, :])
    out_ref[...] = x_ref[...] + gate_ref[...] * _rms(y, gpost_ref[...])


def _outproj_odd(x, ys5, u, d_skip, glu_w, o, w_out, gate, gpost):
    l, d = x.shape
    tm = min(512, l)
    vec = _full((1, d))
    return pl.pallas_call(
        _outproj_odd_kernel,
        grid=(l // tm,),
        in_specs=[_rows(tm, d), _rows(tm, 512), _rows(tm, 512), _full((1, 512)), _full(glu_w.shape),
                  _rows(tm, 512), _full(w_out.shape), vec, vec],
        out_specs=_rows(tm, d),
        out_shape=jax.ShapeDtypeStruct((l, d), F32),
        compiler_params=_cparams(("parallel",)),
        name="outproj_odd",
    )(x, ys5, u, d_skip, glu_w, o, w_out, gate, gpost)


FFN_HALO = 16


def _ffn_kernel(xm_ref, xp_ref, xn_ref, sh_ref, sc_ref, g_ref, gate_ref, gpost_ref,
                wa_ref, wg_ref, cwa_ref, cwg_ref, cba_ref, cbg_ref, wd_ref, out_ref,
                hext_ref, acc_ref, *, tm, nrow, nchunk):
    i = pl.program_id(0)
    c = pl.program_id(1)

    @pl.when(c == 0)
    def _():
        g, sh, sc = g_ref[...], sh_ref[...], sc_ref[...]
        hp = _rms_mod(xp_ref[...], g, sh, sc) * (i > 0).astype(F32)
        hn = _rms_mod(xn_ref[...], g, sh, sc) * (i < nrow - 1).astype(F32)
        hext_ref[0:FFN_HALO, :] = hp.astype(BF16)
        hext_ref[FFN_HALO:FFN_HALO + tm, :] = _rms_mod(xm_ref[...], g, sh, sc).astype(BF16)
        hext_ref[FFN_HALO + tm:2 * FFN_HALO + tm, :] = hn.astype(BF16)
        acc_ref[...] = jnp.zeros(acc_ref.shape, F32)

    he = hext_ref[...]
    n = tm + 2 * FFN_HALO

    def branch(w_ref, cw_ref, cb_ref):
        p = _dot(he, w_ref[...])
        cw = cw_ref[...]
        up = pltpu.roll(p, 1, 0)[FFN_HALO:FFN_HALO + tm]
        dn = pltpu.roll(p, n - 1, 0)[FFN_HALO:FFN_HALO + tm]
        return up * cw[0:1] + p[FFN_HALO:FFN_HALO + tm] * cw[1:2] + dn * cw[2:3] + cb_ref[...]

    ua = branch(wa_ref, cwa_ref, cba_ref)
    ug = branch(wg_ref, cwg_ref, cbg_ref)
    act = (ua * (ug * jax.nn.sigmoid(ug))).astype(BF16)
    acc_ref[...] += _dot(act, wd_ref[...])

    @pl.when(c == nchunk - 1)
    def _():
        out_ref[...] = xm_ref[...] + gate_ref[...] * _rms(acc_ref[...], gpost_ref[...])


def _conv_ffn(x, shift, scale, g, gate, gpost, w_up, conv_w, conv_b, w_down):
    l, d = x.shape
    dff = w_down.shape[0]
    tm = min(512, l)
    tn = 256
    nrow = l // tm
    nchunk = dff // tn
    prev, nxt = _halo_maps(tm, FFN_HALO, l)
    vec = _full((1, d))
    col_a = lambda i, c: (0, c)
    col_g = lambda i, c: (0, c + nchunk)
    return pl.pallas_call(
        functools.partial(_ffn_kernel, tm=tm, nrow=nrow, nchunk=nchunk),
        grid=(nrow, nchunk),
        in_specs=[_rows(tm, d), pl.BlockSpec((FFN_HALO, d), prev), pl.BlockSpec((FFN_HALO, d), nxt),
                  vec, vec, vec, vec, vec,
                  pl.BlockSpec((d, tn), col_a), pl.BlockSpec((d, tn), col_g),
                  pl.BlockSpec((3, tn), col_a), pl.BlockSpec((3, tn), col_g),
                  pl.BlockSpec((1, tn), col_a), pl.BlockSpec((1, tn), col_g),
                  pl.BlockSpec((tn, d), lambda i, c: (c, 0))],
        out_specs=_rows(tm, d),
        out_shape=jax.ShapeDtypeStruct((l, d), F32),
        scratch_shapes=[pltpu.VMEM((tm + 2 * FFN_HALO, d), BF16), pltpu.VMEM((tm, d), F32)],
        compiler_params=_cparams(("parallel", "arbitrary")),
        name="conv_ffn",
    )(x, x, x, shift, scale, g, gate, gpost, w_up, w_up, conv_w, conv_w, conv_b, conv_b, w_down)


S5_GB = 4


def _s5_kernel(zl_ref, zc_ref, w1_ref, wo_ref, a1_ref, a2_ref, y_ref, yi_ref, s_ref, ss_ref, hp_ref,
               *, nlat, nctx):
    nch = nlat + nctx
    gw = 2 * S5_STATE
    for d in range(2):
        lat0 = nctx if d == 0 else 0
        ctx0 = 0 if d == 0 else nlat
        for g in range(S5_GB):
            for z_ref, r0, nr in ((zl_ref, lat0, nlat), (zc_ref, ctx0, nctx)):
                r = _dot(z_ref[g], w1_ref[d, g])
                yi_ref[g, r0:r0 + nr, :] = r[:, 0:256]
                s_ref[r0:r0 + nr, g * gw:(g + 1) * gw] = r[:, 256:384]
                ss_ref[r0:r0 + nr, g * gw:(g + 1) * gw] = r[:, 384:512]
        a1 = a1_ref[d]
        a2 = a2_ref[d]

        def step(t, carry, d=d, a1=a1, a2=a2):
            h, hs = carry
            j = t if d == 0 else nch - 1 - t
            hp_ref[pl.ds(j, 1), :] = h
            s = s_ref[pl.ds(j, 1), :]
            ss = ss_ref[pl.ds(j, 1), :]
            return h * a1 + hs * a2 + s, hs * a1 - h * a2 + ss

        zero = jnp.zeros((1, S5_GB * gw), F32)
        lax.fori_loop(0, nch, step, (zero, zero))
        for g in range(S5_GB):
            hprev = hp_ref[lat0:lat0 + nlat, g * gw:(g + 1) * gw].astype(BF16)
            y = yi_ref[g, lat0:lat0 + nlat, :] + _dot(hprev, wo_ref[d, g])
            if d == 0:
                y_ref[g] = y
            else:
                y_ref[g] += y


def _s5_weights(a_re, a_im, log_dt, b_re, b_im, c_re, c_im):
    t = S5_CHUNK
    hp = lax.Precision.HIGHEST
    dt = jnp.exp(log_dt)[..., None]
    mag = jnp.exp(a_re * dt)
    ab_re = mag * jnp.cos(a_im * dt)
    ab_im = mag * jnp.sin(a_im * dt)
    nr, ni = ab_re - 1.0, ab_im
    den = a_re * a_re + a_im * a_im
    cr = ((nr * a_re + ni * a_im) / den)[..., None]
    ci = ((ni * a_re - nr * a_im) / den)[..., None]
    bb_re = cr * b_re - ci * b_im
    bb_im = cr * b_im + ci * b_re
    k = jnp.arange(t + 1, dtype=F32)[:, None, None, None]
    pmag = jnp.exp(k * (a_re * dt))
    pw_re = pmag * jnp.cos(k * (a_im * dt))
    pw_im = pmag * jnp.sin(k * (a_im * dt))
    ab_k_re = pw_re[..., None] * bb_re - pw_im[..., None] * bb_im
    ab_k_im = pw_re[..., None] * bb_im + pw_im[..., None] * bb_re
    kern = (jnp.einsum('dgcp,kdgph->kdgch', c_re, ab_k_re, precision=hp)
            - jnp.einsum('dgcp,kdgph->kdgch', c_im, ab_k_im, precision=hp))
    ca_re = c_re[None] * pw_re[:, :, :, None, :] - c_im[None] * pw_im[:, :, :, None, :]
    ca_im = c_re[None] * pw_im[:, :, :, None, :] + c_im[None] * pw_re[:, :, :, None, :]
    g = a_re.shape[1]
    sig = jnp.arange(t)[:, None]
    tau = jnp.arange(t)[None, :]
    w1, wo, a1, a2 = [], [], [], []
    for d in range(2):
        lag = (tau - sig) if d == 0 else (sig - tau)
        m = kern[jnp.clip(lag, 0, t), d] * (lag >= 0)[:, :, None, None, None]
        m = m.transpose(2, 0, 4, 1, 3).reshape(g, t * S5_GROUP_CH, t * S5_GROUP_CH)
        pin = (t - 1 - jnp.arange(t)) if d == 0 else jnp.arange(t)
        st_re = ab_k_re[pin, d].transpose(1, 0, 3, 2).reshape(g, t * S5_GROUP_CH, S5_STATE)
        st_im = ab_k_im[pin, d].transpose(1, 0, 3, 2).reshape(g, t * S5_GROUP_CH, S5_STATE)
        w1.append(jnp.concatenate([m, st_re, st_im, st_im, st_re], axis=-1))
        pout = (jnp.arange(t) + 1) if d == 0 else (t - jnp.arange(t))
        o_re = ca_re[pout, d].transpose(1, 3, 0, 2).reshape(g, S5_STATE, t * S5_GROUP_CH)
        o_im = ca_im[pout, d].transpose(1, 3, 0, 2).reshape(g, S5_STATE, t * S5_GROUP_CH)
        wo.append(jnp.concatenate([o_re, -o_im], axis=1))
        ar, ai = pw_re[t, d], pw_im[t, d]
        a1.append(jnp.concatenate([ar, ar], axis=-1).reshape(1, -1))
        a2.append(jnp.concatenate([-ai, ai], axis=-1).reshape(1, -1))
    return (jnp.stack(w1).astype(BF16), jnp.stack(wo).astype(BF16), jnp.stack(a1), jnp.stack(a2))


def _to_group_major(u):
    l, w = u.shape
    g = w // S5_GROUP_CH
    return (u.reshape(l // S5_CHUNK, S5_CHUNK, g, S5_GROUP_CH).transpose(2, 0, 1, 3)
            .reshape(g, l // S5_CHUNK, S5_CHUNK * S5_GROUP_CH))


def _s5_mix(u, uc, a_re, a_im, log_dt, b_re, b_im, c_re, c_im):
    l, w = u.shape
    g = w // S5_GROUP_CH
    w1, wo, a1, a2 = _s5_weights(a_re, a_im, log_dt, b_re, b_im, c_re, c_im)
    zl = _to_group_major(u).astype(BF16)
    zc = _to_group_major(uc).astype(BF16)
    nlat, nctx = zl.shape[1], zc.shape[1]
    nch = nlat + nctx
    cw = S5_CHUNK * S5_GROUP_CH
    gw = 2 * S5_STATE
    y = pl.pallas_call(
        functools.partial(_s5_kernel, nlat=nlat, nctx=nctx),
        grid=(g // S5_GB,),
        in_specs=[pl.BlockSpec((S5_GB, nlat, cw), lambda i: (i, 0, 0)),
                  pl.BlockSpec((S5_GB, nctx, cw), lambda i: (i, 0, 0)),
                  pl.BlockSpec((2, S5_GB, cw, cw + 2 * gw), lambda i: (0, i, 0, 0)),
                  pl.BlockSpec((2, S5_GB, gw, cw), lambda i: (0, i, 0, 0)),
                  pl.BlockSpec((2, 1, S5_GB * gw), lambda i: (0, 0, i)),
                  pl.BlockSpec((2, 1, S5_GB * gw), lambda i: (0, 0, i))],
        out_specs=pl.BlockSpec((S5_GB, nlat, cw), lambda i: (i, 0, 0)),
        out_shape=jax.ShapeDtypeStruct((g, nlat, cw), F32),
        scratch_shapes=[pltpu.VMEM((S5_GB, nch, cw), F32), pltpu.VMEM((nch, S5_GB * gw), F32),
                        pltpu.VMEM((nch, S5_GB * gw), F32), pltpu.VMEM((nch, S5_GB * gw), F32)],
        compiler_params=_cparams(("parallel",)),
        name="s5_mix",
    )(zl, zc, w1, wo, a1, a2)
    return (y.reshape(g, nlat, S5_CHUNK, S5_GROUP_CH).transpose(1, 2, 0, 3).reshape(l, w))


def _swap_pairs(w):
    n = w.shape[-1]
    return w.reshape(w.shape[:-1] + (n // 2, 2))[..., ::-1].reshape(w.shape)


def _rope_tables(rows):
    row = jnp.repeat(jnp.arange(rows, dtype=F32), GRID_W)
    col = jnp.tile(jnp.arange(GRID_W, dtype=F32), rows)
    inv = ROPE_THETA ** (-jnp.arange(ROPE_FREQS, dtype=F32) / ROPE_FREQS)
    ang = jnp.concatenate([row[:, None] * inv, col[:, None] * inv], axis=-1)
    return jnp.cos(ang), jnp.sin(ang)


def _rotary_pair_tables(cos, sin, gain):
    cos_rep = jnp.repeat(cos, 2, axis=-1)
    sin_rep = jnp.repeat(sin, 2, axis=-1)
    sign = jnp.tile(jnp.array([-1.0, 1.0], F32), HEAD_DIM // 2)
    c = cos_rep * gain
    s = sin_rep * sign * _swap_pairs(gain)
    return jnp.tile(c, (1, 2)), jnp.tile(s, (1, 2))


def _block_diag_ones(n, blk):
    idx = jnp.arange(n) // blk
    return (idx[:, None] == idx[None, :]).astype(BF16)


def _dup_heads(k, nheads):
    s = k.shape[0]
    kh = k.reshape(s, nheads, HEAD_DIM).transpose(1, 0, 2)
    return jnp.concatenate([kh, kh], axis=-1)


def _ones_heads(v, nheads):
    s = v.shape[0]
    vh = v.reshape(s, nheads, v.shape[1] // nheads).transpose(1, 0, 2)
    return jnp.concatenate([vh, jnp.ones_like(vh)], axis=-1)


def kernel(x, c, ctx, c_ctx, mod_w, mod_b, norm_pre, norm_post, ffn_w_up, ffn_conv_w, ffn_conv_b, ffn_w_down,
           ev_w_in, ev_conv_w, ev_q_norm, ev_k_norm, ev_w_out,
           od_w_in, od_s5_a_re, od_s5_a_im, od_s5_log_dt, od_s5_b_re, od_s5_b_im,
           od_s5_c_re, od_s5_c_im, od_s5_d, od_glu_w,
           od_lam_q1, od_lam_k1, od_lam_q2, od_lam_k2, od_subln, od_w_out):
    b, l, d = x.shape
    assert b == 1, "single-sequence kernel"
    depth = mod_w.shape[0]
    cl = ctx.shape[1]
    xs, xc = x[0], ctx[0]
    cos, sin = _rope_tables(l // GRID_W)
    ones_c, zeros_c = jnp.ones((cl, HEAD_DIM // 2), F32), jnp.zeros((cl, HEAD_DIM // 2), F32)
    bd = _block_diag_ones(512, HEAD_DIM)
    mods = _mod_vectors(c, c_ctx, mod_w, mod_b)

    def vecs(i, row):
        m = mods[i, row].reshape(6, 1, d)
        return [m[j] for j in range(6)]

    def ffn(xx, i, m):
        return _conv_ffn(xx, m[3], m[4], norm_pre[i, 1][None], m[5], norm_post[i, 1][None],
                         ffn_w_up[i].astype(BF16), ffn_conv_w[i], ffn_conv_b[i][None], ffn_w_down[i].astype(BF16))

    for i in range(depth):
        ctx_out = i < depth - 1
        j = i // 2
        m, mc = vecs(i, 0), vecs(i, 1)
        g_pre, g_post = norm_pre[i, 0][None], norm_post[i, 0][None]
        if i % 2 == 0:
            w = ev_w_in[j]
            w_ext = jnp.concatenate([w, _swap_pairs(w[:, 1536:2048]), _swap_pairs(w[:, 2048:2176])],
                                    axis=1).astype(BF16)
            w_out = ev_w_out[j].astype(BF16)
            one = jnp.ones((HEAD_DIM,), F32)
            cq, sq = _rotary_pair_tables(cos, sin, ev_q_norm[j])
            ck, sk = _rotary_pair_tables(cos, sin, ev_k_norm[j])
            ccq, csq = _rotary_pair_tables(ones_c, zeros_c, ev_q_norm[j])
            cck, csk = _rotary_pair_tables(ones_c, zeros_c, ev_k_norm[j])
            gb, gchv, q, k, v = _inproj_even(xs, m[0], m[1], g_pre, w_ext, bd, cq, sq, ck, sk)
            cgb, cgchv, qc, kc, vc = _inproj_even(xc, mc[0], mc[1], g_pre, w_ext, bd, ccq, csq, cck, csk)
            kall = _dup_heads(jnp.concatenate([kc, k], axis=0), KV_HEADS)
            vall = _ones_heads(jnp.concatenate([vc, v], axis=0), KV_HEADS)
            o = _gqa_attention(q, kall, vall)
            xs_new = _outproj_even(xs, gb, gchv, ev_conv_w[j], o, w_out, m[2], g_post)
            if ctx_out:
                oc = _gqa_attention(qc, _dup_heads(kc, KV_HEADS), _ones_heads(vc, KV_HEADS))
                xc = _outproj_even(xc, cgb, cgchv, ev_conv_w[j], oc, w_out, mc[2], g_post)
            xs = xs_new
        else:
            w = od_w_in[j]
            w_ext = jnp.concatenate([w, _swap_pairs(w[:, 512:1024]), _swap_pairs(w[:, 1024:1536])],
                                    axis=1).astype(BF16)
            one = jnp.ones((HEAD_DIM,), F32)
            ct, st = _rotary_pair_tables(cos, sin, one)
            cct, cst = _rotary_pair_tables(ones_c, zeros_c, one)
            u, q, k, v = _inproj_odd(xs, m[0], m[1], g_pre, w_ext, ct, st)
            uc, qc, kc, vc = _inproj_odd(xc, mc[0], mc[1], g_pre, w_ext, cct, cst)
            ys5 = _s5_mix(u, uc, od_s5_a_re[j], od_s5_a_im[j], od_s5_log_dt[j], od_s5_b_re[j], od_s5_b_im[j],
                          od_s5_c_re[j], od_s5_c_im[j])
            lam_init = 0.8 - 0.6 * math.exp(-0.3 * i)
            vall = jnp.concatenate([vc, v], axis=0).reshape(cl + l, DIFF_HEADS, 2 * HEAD_DIM)
            vext = jnp.concatenate([vall, jnp.ones_like(vall)], axis=-1).reshape(cl + l, -1)
            o = _diff_attention(q, jnp.concatenate([kc, k], axis=0), vext,
                                od_lam_q1[j][None], od_lam_k1[j][None], od_lam_q2[j][None], od_lam_k2[j][None],
                                od_subln[j][None], lam_init)
            if ctx_out:
                raise NotImplementedError("context stream after an S5 | diff-attention layer")
            xs = _outproj_odd(xs, ys5, u, od_s5_d[j][None], od_glu_w[j].astype(BF16), o,
                              od_w_out[j].astype(BF16), m[2], g_post)
        xs = ffn(xs, i, m)
        if ctx_out:
            xc = ffn(xc, i, mc)
    return xs[None]
```
